```python
import jax, jax.numpy as jnp
from jax import lax
import numpy as np

D_MODEL = 1024
BATCH = 4
SEQ = 8192
DEPTH = 2

N_META = 16
BLOCK_Q = 128
N_HEADS_SB = 8
N_HEADS_FOX = 8
HEAD_DIM = D_MODEL // (N_HEADS_SB + N_HEADS_FOX)
D_SB = N_HEADS_SB * HEAD_DIM
D_FOX = N_HEADS_FOX * HEAD_DIM
D_IN_ATTN = 3 * D_SB + 3 * D_FOX + N_HEADS_FOX
D_FF = 2816
POOL_WINDOWS = (2, 4, 8, 16)
N_POOL_GROUPS = len(POOL_WINDOWS)
POOL_GROUP = D_MODEL // N_POOL_GROUPS
RMS_EPS = 1e-6
N_EVEN = (DEPTH + 1) // 2
N_ODD = DEPTH // 2

kernel_name = "hybrid_stickbreak_fox_pool_macaron"


def rms_norm(x, g):
    xf = x.astype(jnp.float32)
    y = xf * lax.rsqrt(jnp.mean(xf * xf, axis=-1, keepdims=True) + RMS_EPS)
    return (y * g.astype(jnp.float32)).astype(x.dtype)


def swiglu(h, w_in, w_out):
    u = h @ w_in
    gate, up = u[..., :D_FF], u[..., D_FF:]
    return (jax.nn.silu(gate) * up) @ w_out


def attn_mixer(h, w_in, b_f, w_out):
    B, L, _ = h.shape
    proj = h @ w_in
    offs = np.cumsum([0, D_SB, D_SB, D_SB, D_FOX, D_FOX, D_FOX])

    def heads(i, n):
        t = proj[..., offs[i]:offs[i + 1]]
        return t.reshape(B, L, n, HEAD_DIM).transpose(0, 2, 1, 3)

    qa, ka, va = heads(0, N_HEADS_SB), heads(1, N_HEADS_SB), heads(2, N_HEADS_SB)
    qb, kb, vb = heads(3, N_HEADS_FOX), heads(4, N_HEADS_FOX), heads(5, N_HEADS_FOX)
    f_logit = proj[..., offs[6]:]
    log_f = jax.nn.log_sigmoid(f_logit.astype(jnp.float32) + b_f.astype(jnp.float32))
    c = jnp.cumsum(log_f, axis=1).transpose(0, 2, 1)

    scale = HEAD_DIM ** -0.5
    starts = [0] + list(range(N_META, L, BLOCK_Q))
    ends = starts[1:] + [L]
    out_sb, out_fox = [], []
    for q0, q1 in zip(starts, ends):
        tpos = jnp.arange(q0, q1)[:, None]
        spos = jnp.arange(q1)[None, :]
        strict = spos < tpos
        causal = spos <= tpos

        z = jnp.einsum('bhqd,bhkd->bhqk', qa[:, :, q0:q1], ka[:, :, :q1]).astype(jnp.float32) * scale
        log_not = jnp.where(strict, jax.nn.log_sigmoid(-z), 0.0)
        rest = lax.cumsum(log_not, axis=3, reverse=True) - log_not
        a_w = jnp.where(strict, jnp.exp(jax.nn.log_sigmoid(z) + rest), 0.0)
        out_sb.append(jnp.einsum('bhqk,bhkd->bhqd', a_w.astype(va.dtype), va[:, :, :q1]))

        s = jnp.einsum('bhqd,bhkd->bhqk', qb[:, :, q0:q1], kb[:, :, :q1]).astype(jnp.float32) * scale
        s = s + c[:, :, q0:q1, None] - c[:, :, None, :q1]
        p = jax.nn.softmax(jnp.where(causal, s, -jnp.inf), axis=-1)
        out_fox.append(jnp.einsum('bhqk,bhkd->bhqd', p.astype(vb.dtype), vb[:, :, :q1]))

    o_sb = jnp.concatenate(out_sb, axis=2).transpose(0, 2, 1, 3).reshape(B, L, D_SB)
    o_fox = jnp.concatenate(out_fox, axis=2).transpose(0, 2, 1, 3).reshape(B, L, D_FOX)
    return jnp.concatenate([o_sb, o_fox], axis=-1) @ w_out


def pool_mixer(h, w_pool, scale):
    B, L, _ = h.shape
    hf = h.astype(jnp.float32).reshape(B, L, N_POOL_GROUPS, POOL_GROUP)
    cs0 = jnp.concatenate([jnp.zeros((B, 1, N_POOL_GROUPS, POOL_GROUP), jnp.float32),
                           jnp.cumsum(hf, axis=1)], axis=1)
    t = jnp.arange(L)
    pooled = []
    for g, w in enumerate(POOL_WINDOWS):
        cg = cs0[:, :, g]
        upper = cg[:, 1:]
        lower = jnp.concatenate([jnp.zeros((B, w - 1, POOL_GROUP), jnp.float32),
                                 cg[:, :L - w + 1]], axis=1)
        cnt = jnp.minimum(t + 1, w).astype(jnp.float32)[None, :, None]
        pooled.append((upper - lower) / cnt - hf[:, :, g])
    p = jnp.stack(pooled, axis=2).astype(h.dtype)
    y = jnp.einsum('blgc,gcd->blgd', p, w_pool).reshape(B, L, D_MODEL)
    return y * scale


def setup_inputs(seed: int = 0) -> dict:
    key = jax.random.key(seed)
    ks = jax.random.split(key, 20)
    f32 = jnp.float32
    nrm = lambda k, shp, s: jax.random.normal(k, shp, f32) * s
    return {
        "x": nrm(ks[0], (BATCH, SEQ, D_MODEL), 1.0),
        "meta_tokens": nrm(ks[1], (N_META, D_MODEL), 1.0),
        "g_ffn1": 1.0 + nrm(ks[2], (DEPTH, D_MODEL), 0.02),
        "w_ffn1_in": nrm(ks[3], (DEPTH, D_MODEL, 2 * D_FF), D_MODEL ** -0.5),
        "w_ffn1_out": nrm(ks[4], (DEPTH, D_FF, D_MODEL), D_FF ** -0.5),
        "g_mix": 1.0 + nrm(ks[5], (DEPTH, D_MODEL), 0.02),
        "g_ffn2": 1.0 + nrm(ks[6], (DEPTH, D_MODEL), 0.02),
        "w_ffn2_in": nrm(ks[7], (DEPTH, D_MODEL, 2 * D_FF), D_MODEL ** -0.5),
        "w_ffn2_out": nrm(ks[8], (DEPTH, D_FF, D_MODEL), D_FF ** -0.5),
        "w_attn_in": nrm(ks[9], (N_EVEN, D_MODEL, D_IN_ATTN), D_MODEL ** -0.5),
        "b_forget": 2.0 + nrm(ks[10], (N_EVEN, N_HEADS_FOX), 0.1),
        "w_attn_out": nrm(ks[11], (N_EVEN, D_SB + D_FOX, D_MODEL), (D_SB + D_FOX) ** -0.5),
        "w_pool": nrm(ks[12], (N_ODD, N_POOL_GROUPS, POOL_GROUP, POOL_GROUP), POOL_GROUP ** -0.5),
        "pool_scale": 1.0 + nrm(ks[13], (N_ODD, D_MODEL), 0.02),
        "g_final": 1.0 + nrm(ks[14], (D_MODEL,), 0.02),
    }


def reference(x, meta_tokens, g_ffn1, w_ffn1_in, w_ffn1_out, g_mix, g_ffn2,
              w_ffn2_in, w_ffn2_out, w_attn_in, b_forget, w_attn_out,
              w_pool, pool_scale, g_final):
    B = x.shape[0]
    meta = jnp.broadcast_to(meta_tokens[None].astype(x.dtype), (B, N_META, D_MODEL))
    h = jnp.concatenate([meta, x], axis=1)
    for layer in range(DEPTH):
        h = h + 0.5 * swiglu(rms_norm(h, g_ffn1[layer]), w_ffn1_in[layer], w_ffn1_out[layer])
        u = rms_norm(h, g_mix[layer])
        i = layer // 2
        if layer % 2 == 0:
            h = h + attn_mixer(u, w_attn_in[i], b_forget[i], w_attn_out[i])
        else:
            h = h + pool_mixer(u, w_pool[i], pool_scale[i])
        h = h + 0.5 * swiglu(rms_norm(h, g_ffn2[layer]), w_ffn2_in[layer], w_ffn2_out[layer])
    return rms_norm(h, g_final)[:, N_META:]
```

```python
import functools

import jax
import jax.numpy as jnp
from jax import lax
from jax.experimental import pallas as pl
from jax.experimental.pallas import tpu as pltpu

N_META = 16
N_HEADS_SB = 8
N_HEADS_FOX = 8
HEAD_DIM = 64
POOL_WINDOWS = (2, 4, 8, 16)
POOL_HALO = 16
RMS_EPS = 1e-6

BLK = 256
HALF = BLK // 2
PAIR = 2 * HEAD_DIM
FFN_CHUNK = 256
VMEM_LIMIT = 56 * 1024 * 1024

f32 = jnp.float32
bf16 = jnp.bfloat16
NT_DIMS = (((1,), (1,)), ((), ()))
TN_DIMS = (((0,), (0,)), ((), ()))


def _rms(x, g):
    return (x * lax.rsqrt(jnp.mean(x * x, axis=-1, keepdims=True) + RMS_EPS)) * g


def _log_sigmoid(z):
    return jnp.minimum(z, 0.0) - jnp.log1p(jnp.exp(-jnp.abs(z)))


def _split2(x):
    hi = x.astype(bf16)
    lo = (x - hi.astype(f32)).astype(bf16)
    return hi, lo


def _split3(x):
    hi = x.astype(bf16)
    r = x - hi.astype(f32)
    mid = r.astype(bf16)
    lo = (r - mid.astype(f32)).astype(bf16)
    return hi, mid, lo


def _dot(a, b):
    return jnp.dot(a, b, preferred_element_type=f32)


def _ffn_kernel(x_ref, g_ref, win_ref, wout_ref, o_ref, u_ref, acc_ref, *, n_chunks, tf):
    x = x_ref[...]
    u_ref[...] = _rms(x, g_ref[...]).astype(bf16)
    acc_ref[...] = jnp.zeros_like(acc_ref)

    def body(c, carry):
        gu = _dot(u_ref[...], win_ref[c])
        gate = gu[:, :tf]
        act = (gate * jax.nn.sigmoid(gate)) * gu[:, tf:]
        acc_ref[...] += _dot(act.astype(bf16), wout_ref[c])
        return carry

    lax.fori_loop(0, n_chunks, body, 0)
    o_ref[...] = x + 0.5 * acc_ref[...]


def _ffn(h2d, g, win_p, wout_p):
    T, D = h2d.shape
    n_chunks, _, tf2 = win_p.shape
    tf = tf2 // 2
    tm = 512 if T % 512 == 0 else BLK
    const = dict(pipeline_mode=pl.Buffered(1))
    return pl.pallas_call(
        functools.partial(_ffn_kernel, n_chunks=n_chunks, tf=tf),
        grid=(T // tm,),
        in_specs=[
            pl.BlockSpec((tm, D), lambda i: (i, 0)),
            pl.BlockSpec((1, D), lambda i: (0, 0)),
            pl.BlockSpec((n_chunks, D, tf2), lambda i: (0, 0, 0), **const),
            pl.BlockSpec((n_chunks, tf, D), lambda i: (0, 0, 0), **const),
        ],
        out_specs=pl.BlockSpec((tm, D), lambda i: (i, 0)),
        out_shape=jax.ShapeDtypeStruct((T, D), f32),
        scratch_shapes=[pltpu.VMEM((tm, D), bf16), pltpu.VMEM((tm, D), f32)],
        compiler_params=pltpu.CompilerParams(
            dimension_semantics=("arbitrary",), vmem_limit_bytes=VMEM_LIMIT),
        name="ffn",
    )(h2d, g, win_p, wout_p)


def _pack_ffn(w_in, w_out):
    D, two_f = w_in.shape
    F = two_f // 2
    n = F // FFN_CHUNK
    gate = w_in[:, :F].reshape(D, n, FFN_CHUNK)
    up = w_in[:, F:].reshape(D, n, FFN_CHUNK)
    win_p = jnp.concatenate([gate, up], axis=2).transpose(1, 0, 2).astype(bf16)
    wout_p = w_out.reshape(n, FFN_CHUNK, D).astype(bf16)
    return win_p, wout_p


def _attn_in_kernel(h_ref, g_ref, wk_ref, wqvt_ref, wf_ref, bf_ref,
                    k_ref, qvt_ref, kaug_ref, qaugt_ref, carry_ref):
    @pl.when(pl.program_id(1) == 0)
    def _():
        carry_ref[...] = jnp.zeros_like(carry_ref)

    u = _rms(h_ref[...], g_ref[...]).astype(bf16)
    k_ref[...] = _dot(u, wk_ref[...]).astype(bf16)
    qvt_ref[...] = lax.dot_general(wqvt_ref[...], u, NT_DIMS,
                                   preferred_element_type=f32).astype(bf16)

    lf = _log_sigmoid(_dot(u, wf_ref[...]) + bf_ref[...])
    r = lax.broadcasted_iota(jnp.int32, (BLK, BLK), 0)
    c = lax.broadcasted_iota(jnp.int32, (BLK, BLK), 1)
    ltri = jnp.where(r >= c, 1.0, 0.0).astype(bf16)
    hi, mid, lo = _split3(lf)
    cs = (_dot(ltri, hi) + _dot(ltri, mid)) + _dot(ltri, lo) + carry_ref[0:1, :]
    carry_ref[0:1, :] = cs[BLK - 1:BLK, :]

    chi, cmid, clo = _split3(cs)
    pr = lax.broadcasted_iota(jnp.int32, (PAIR, PAIR), 0)
    pc = lax.broadcasted_iota(jnp.int32, (PAIR, PAIR), 1)

    def place_k(k):
        return jnp.where((pc == 8 * pr + k) & (pr < N_HEADS_FOX), 1.0, 0.0).astype(bf16)

    def place_q(k):
        return jnp.where((pr == 8 * pc + k) & (pc < N_HEADS_FOX), 1.0, 0.0).astype(bf16)

    lane = lax.broadcasted_iota(jnp.int32, (1, PAIR), 1)
    ones_k = jnp.where(((lane & 7) < 3) & (lane < 8 * N_HEADS_FOX), 1.0, 0.0)
    kaug = ones_k - ((_dot(chi, place_k(3)) + _dot(cmid, place_k(4))) + _dot(clo, place_k(5)))
    kaug_ref[...] = kaug.astype(bf16)

    row = lax.broadcasted_iota(jnp.int32, (PAIR, 1), 0)
    ones_q = jnp.where(((row & 7) >= 3) & ((row & 7) < 6) & (row < 8 * N_HEADS_FOX), 1.0, 0.0)
    dg = functools.partial(lax.dot_general, dimension_numbers=NT_DIMS,
                           preferred_element_type=f32)
    qaugt = ones_q + ((dg(place_q(0), chi) + dg(place_q(1), cmid)) + dg(place_q(2), clo))
    qaugt_ref[...] = qaugt.astype(bf16)


def _attn_in(h, g, wk, wqvt, wf, bfp):
    B, Lp, D = h.shape
    nblk = Lp // BLK
    nk = wk.shape[1]
    nqv = wqvt.shape[0]
    const = dict(pipeline_mode=pl.Buffered(1))
    return pl.pallas_call(
        _attn_in_kernel,
        grid=(B, nblk),
        in_specs=[
            pl.BlockSpec((None, BLK, D), lambda b, j: (b, j, 0)),
            pl.BlockSpec((1, D), lambda b, j: (0, 0)),
            pl.BlockSpec((D, nk), lambda b, j: (0, 0), **const),
            pl.BlockSpec((nqv, D), lambda b, j: (0, 0), **const),
            pl.BlockSpec((D, PAIR), lambda b, j: (0, 0), **const),
            pl.BlockSpec((1, PAIR), lambda b, j: (0, 0)),
        ],
        out_specs=[
            pl.BlockSpec((None, BLK, nk), lambda b, j: (b, j, 0)),
            pl.BlockSpec((None, None, nqv, BLK), lambda b, j: (b, j, 0, 0)),
            pl.BlockSpec((None, BLK, PAIR), lambda b, j: (b, j, 0)),
            pl.BlockSpec((None, None, PAIR, BLK), lambda b, j: (b, j, 0, 0)),
        ],
        out_shape=[
            jax.ShapeDtypeStruct((B, Lp, nk), bf16),
            jax.ShapeDtypeStruct((B, nblk, nqv, BLK), bf16),
            jax.ShapeDtypeStruct((B, Lp, PAIR), bf16),
            jax.ShapeDtypeStruct((B, nblk, PAIR, BLK), bf16),
        ],
        scratch_shapes=[pltpu.VMEM((8, PAIR), f32)],
        compiler_params=pltpu.CompilerParams(
            dimension_semantics=("arbitrary", "arbitrary"), vmem_limit_bytes=VMEM_LIMIT),
        name="attn_in",
    )(h, g, wk, wqvt, wf, bfp)


def _sb_weights(zt, r_prev, uu, valid):
    ls = _log_sigmoid(zt)
    ln = ls - zt
    if valid is not None:
        ln = jnp.where(valid, ln, 0.0)
    hi, lo = _split2(ln)
    cum_b = _dot(uu, jnp.concatenate([hi[HALF:], lo[HALF:]], axis=0))
    tot_b = cum_b[0:1] + ln[HALF:HALF + 1]
    cum_a = _dot(uu, jnp.concatenate([hi[:HALF], lo[:HALF]], axis=0)) + tot_b
    tot = cum_a[0:1] + ln[0:1]
    rest = jnp.concatenate([cum_a, cum_b], axis=0) + r_prev
    a = jnp.exp(ls + rest)
    if valid is not None:
        a = jnp.where(valid, a, 0.0)
    return a.astype(bf16), r_prev + tot


def _sb_kernel(k_ref, qt_ref, vt_ref, o_ref):
    i = pl.program_id(2)
    qt = qt_ref[...]
    prow = lax.broadcasted_iota(jnp.int32, (PAIR, BLK), 0)
    zero = jnp.zeros_like(qt)
    qts = (jnp.where(prow < HEAD_DIM, qt, zero), jnp.where(prow >= HEAD_DIM, qt, zero))
    ur = lax.broadcasted_iota(jnp.int32, (HALF, BLK), 0)
    uc = lax.broadcasted_iota(jnp.int32, (HALF, BLK), 1)
    uu = jnp.where((uc & (HALF - 1)) > ur, 1.0, 0.0).astype(bf16)
    srow = lax.broadcasted_iota(jnp.int32, (BLK, BLK), 0)
    tcol = lax.broadcasted_iota(jnp.int32, (BLK, BLK), 1)
    strict = srow < tcol

    def block(j, state, valid):
        kblk = k_ref[pl.ds(pl.multiple_of(j * BLK, BLK), BLK), :]
        vblk = vt_ref[j]
        new = []
        for hh in range(2):
            r_prev, acc = state[hh]
            a, r_new = _sb_weights(_dot(kblk, qts[hh]), r_prev, uu, valid)
            acc = acc + _dot(vblk[hh * HEAD_DIM:(hh + 1) * HEAD_DIM], a)
            new.append((r_new, acc))
        return tuple(new)

    init = tuple((jnp.zeros((1, BLK), f32), jnp.zeros((HEAD_DIM, BLK), f32)) for _ in range(2))
    state = block(i, init, strict)
    state = lax.fori_loop(1, i + 1, lambda jj, st: block(i - jj, st, None), state)
    o_ref[...] = jnp.concatenate([state[0][1], state[1][1]], axis=0).astype(bf16)


def _sb_attn(k_all, qvt):
    B, Lp, _ = k_all.shape
    nblk = Lp // BLK
    npairs = N_HEADS_SB // 2
    return pl.pallas_call(
        _sb_kernel,
        grid=(B, npairs, nblk),
        in_specs=[
            pl.BlockSpec((None, Lp, PAIR), lambda b, p, i: (b, 0, p)),
            pl.BlockSpec((None, None, PAIR, BLK), lambda b, p, i: (b, i, p, 0)),
            pl.BlockSpec((None, nblk, PAIR, BLK), lambda b, p, i: (b, 0, npairs + p, 0)),
        ],
        out_specs=pl.BlockSpec((None, None, PAIR, BLK), lambda b, p, i: (b, i, p, 0)),
        out_shape=jax.ShapeDtypeStruct((B, nblk, npairs * PAIR, BLK), bf16),
        compiler_params=pltpu.CompilerParams(
            dimension_semantics=("arbitrary", "arbitrary", "arbitrary"),
            vmem_limit_bytes=VMEM_LIMIT),
        name="sb_attn",
    )(k_all, qvt, qvt)


def _fox_kernel(k_ref, kaug_ref, qt_ref, qaugt_ref, vt_ref, o_ref):
    p = pl.program_id(1)
    i = pl.program_id(2)
    qt = qt_ref[...]
    qaugt = qaugt_ref[...]
    prow = lax.broadcasted_iota(jnp.int32, (PAIR, BLK), 0)
    zero = jnp.zeros_like(qt)
    rhs = []
    for hh in range(2):
        q_h = jnp.where((prow >= hh * HEAD_DIM) & (prow < (hh + 1) * HEAD_DIM), qt, zero)
        lo_row = 8 * (2 * p + hh)
        b_h = jnp.where((prow >= lo_row) & (prow < lo_row + 6), qaugt, zero)
        rhs.append(jnp.concatenate([q_h, b_h], axis=0))
    ones = jnp.ones((16, BLK), bf16)
    srow = lax.broadcasted_iota(jnp.int32, (BLK, BLK), 0)
    tcol = lax.broadcasted_iota(jnp.int32, (BLK, BLK), 1)
    causal = srow <= tcol

    def block(j, state, valid):
        rows = pl.ds(pl.multiple_of(j * BLK, BLK), BLK)
        lhs = jnp.concatenate([k_ref[rows, :], kaug_ref[rows, :]], axis=1)
        vblk = vt_ref[j]
        new = []
        for hh in range(2):
            m_prev, acc = state[hh]
            st = _dot(lhs, rhs[hh])
            if valid is not None:
                st = jnp.where(valid, st, -jnp.inf)
            m_new = jnp.maximum(m_prev, jnp.max(st, axis=0, keepdims=True))
            pt = jnp.exp(st - m_new).astype(bf16)
            vl = jnp.concatenate([vblk[hh * HEAD_DIM:(hh + 1) * HEAD_DIM], ones], axis=0)
            acc = jnp.exp(m_prev - m_new) * acc + _dot(vl, pt)
            new.append((m_new, acc))
        return tuple(new)

    init = tuple((jnp.full((1, BLK), -jnp.inf, f32), jnp.zeros((HEAD_DIM + 16, BLK), f32))
                 for _ in range(2))
    state = block(i, init, causal)
    state = lax.fori_loop(1, i + 1, lambda jj, st: block(i - jj, st, None), state)
    outs = [acc[:HEAD_DIM] / acc[HEAD_DIM:HEAD_DIM + 1] for _, acc in state]
    o_ref[...] = jnp.concatenate(outs, axis=0).astype(bf16)


def _fox_attn(k_all, kaug, qvt, qaugt):
    B, Lp, _ = k_all.shape
    nblk = Lp // BLK
    npairs = N_HEADS_FOX // 2
    k_off = N_HEADS_SB // 2
    q_off = 2 * (N_HEADS_SB // 2)
    v_off = q_off + npairs
    return pl.pallas_call(
        _fox_kernel,
        grid=(B, npairs, nblk),
        in_specs=[
            pl.BlockSpec((None, Lp, PAIR), lambda b, p, i: (b, 0, k_off + p)),
            pl.BlockSpec((None, Lp, PAIR), lambda b, p, i: (b, 0, 0)),
            pl.BlockSpec((None, None, PAIR, BLK), lambda b, p, i: (b, i, q_off + p, 0)),
            pl.BlockSpec((None, None, PAIR, BLK), lambda b, p, i: (b, i, 0, 0)),
            pl.BlockSpec((None, nblk, PAIR, BLK), lambda b, p, i: (b, 0, v_off + p, 0)),
        ],
        out_specs=pl.BlockSpec((None, None, PAIR, BLK), lambda b, p, i: (b, i, p, 0)),
        out_shape=jax.ShapeDtypeStruct((B, nblk, npairs * PAIR, BLK), bf16),
        compiler_params=pltpu.CompilerParams(
            dimension_semantics=("arbitrary", "arbitrary", "arbitrary"),
            vmem_limit_bytes=VMEM_LIMIT),
        name="fox_attn",
    )(k_all, kaug, qvt, qaugt, qvt)


def _attn_out_kernel(h_ref, osb_ref, ofox_ref, wsb_ref, wfox_ref, o_ref):
    dg = functools.partial(lax.dot_general, dimension_numbers=TN_DIMS,
                           preferred_element_type=f32)
    o_ref[...] = h_ref[...] + (dg(osb_ref[...], wsb_ref[...]) + dg(ofox_ref[...], wfox_ref[...]))


def _attn_out(h, ot_sb, ot_fox, w_sb, w_fox):
    B, Lp, D = h.shape
    nblk = Lp // BLK
    dsb, dfox = w_sb.shape[0], w_fox.shape[0]
    const = dict(pipeline_mode=pl.Buffered(1))
    return pl.pallas_call(
        _attn_out_kernel,
        grid=(B, nblk),
        in_specs=[
            pl.BlockSpec((None, BLK, D), lambda b, j: (b, j, 0)),
            pl.BlockSpec((None, None, dsb, BLK), lambda b, j: (b, j, 0, 0)),
            pl.BlockSpec((None, None, dfox, BLK), lambda b, j: (b, j, 0, 0)),
            pl.BlockSpec((dsb, D), lambda b, j: (0, 0), **const),
            pl.BlockSpec((dfox, D), lambda b, j: (0, 0), **const),
        ],
        out_specs=pl.BlockSpec((None, BLK, D), lambda b, j: (b, j, 0)),
        out_shape=jax.ShapeDtypeStruct((B, Lp, D), f32),
        compiler_params=pltpu.CompilerParams(
            dimension_semantics=("arbitrary", "arbitrary"), vmem_limit_bytes=VMEM_LIMIT),
        name="attn_out",
    )(h, ot_sb, ot_fox, w_sb, w_fox)


def _pool_kernel(h_ref, g_ref, wp_ref, sc_ref, o_ref, ext_ref):
    j = pl.program_id(1)

    @pl.when(j == 0)
    def _():
        ext_ref[0:POOL_HALO, :] = jnp.zeros((POOL_HALO, ext_ref.shape[1]), f32)

    @pl.when(j > 0)
    def _():
        ext_ref[0:POOL_HALO, :] = ext_ref[BLK:BLK + POOL_HALO, :]

    x = h_ref[...]
    u = _rms(x, g_ref[...])
    ext_ref[POOL_HALO:, :] = u
    cg = u.shape[1] // len(POOL_WINDOWS)
    pos = j * BLK + lax.broadcasted_iota(jnp.int32, (BLK, cg), 0)
    ys = []
    for gi, w in enumerate(POOL_WINDOWS):
        cols = slice(gi * cg, (gi + 1) * cg)
        ug = u[:, cols]
        acc = ug
        for d in range(1, w):
            acc = acc + ext_ref[POOL_HALO - d:POOL_HALO - d + BLK, cols]
        cnt = jnp.minimum(pos + 1, w).astype(f32)
        pooled = acc / cnt - ug
        ys.append(_dot(pooled.astype(bf16), wp_ref[gi]))
    o_ref[...] = x + jnp.concatenate(ys, axis=1) * sc_ref[...]


def _pool(h, g, wp, sc):
    B, Lp, D = h.shape
    nblk = Lp // BLK
    ng, cg, _ = wp.shape
    return pl.pallas_call(
        _pool_kernel,
        grid=(B, nblk),
        in_specs=[
            pl.BlockSpec((None, BLK, D), lambda b, j: (b, j, 0)),
            pl.BlockSpec((1, D), lambda b, j: (0, 0)),
            pl.BlockSpec((ng, cg, cg), lambda b, j: (0, 0, 0)),
            pl.BlockSpec((1, D), lambda b, j: (0, 0)),
        ],
        out_specs=pl.BlockSpec((None, BLK, D), lambda b, j: (b, j, 0)),
        out_shape=jax.ShapeDtypeStruct((B, Lp, D), f32),
        scratch_shapes=[pltpu.VMEM((BLK + POOL_HALO, D), f32)],
        compiler_params=pltpu.CompilerParams(
            dimension_semantics=("arbitrary", "arbitrary"), vmem_limit_bytes=VMEM_LIMIT),
        name="pool_mixer",
    )(h, g, wp, sc)


def _final_kernel(h_ref, tail_ref, g_ref, o_ref):
    x = jnp.concatenate([h_ref[N_META:, :], tail_ref[...]], axis=0)
    o_ref[...] = _rms(x, g_ref[...])


def _final(h, g, seq):
    B, Lp, D = h.shape
    per = BLK // N_META
    h16 = h.reshape(B, Lp // N_META, N_META, D)
    return pl.pallas_call(
        _final_kernel,
        grid=(B, seq // BLK),
        in_specs=[
            pl.BlockSpec((None, BLK, D), lambda b, j: (b, j, 0)),
            pl.BlockSpec((None, None, N_META, D), lambda b, j: (b, (j + 1) * per, 0, 0)),
            pl.BlockSpec((1, D), lambda b, j: (0, 0)),
        ],
        out_specs=pl.BlockSpec((None, BLK, D), lambda b, j: (b, j, 0)),
        out_shape=jax.ShapeDtypeStruct((B, seq, D), f32),
        compiler_params=pltpu.CompilerParams(
            dimension_semantics=("arbitrary", "arbitrary"), vmem_limit_bytes=VMEM_LIMIT),
        name="final_norm",
    )(h, h16, g)


def kernel(x, meta_tokens, g_ffn1, w_ffn1_in, w_ffn1_out, g_mix, g_ffn2, w_ffn2_in, w_ffn2_out, w_attn_in, b_forget, w_attn_out, w_pool, pool_scale, g_final):
    B, S, D = x.shape
    depth = g_ffn1.shape[0]
    d_sb = N_HEADS_SB * HEAD_DIM
    d_fox = N_HEADS_FOX * HEAD_DIM
    assert S % BLK == 0 and D == d_sb + d_fox
    L = N_META + S
    Lp = -(-L // BLK) * BLK

    meta = jnp.broadcast_to(meta_tokens[None].astype(x.dtype), (B, N_META, D))
    h = jnp.concatenate([meta, x, jnp.zeros((B, Lp - L, D), x.dtype)], axis=1)

    def ffn(h, g, w_in, w_out):
        win_p, wout_p = _pack_ffn(w_in, w_out)
        return _ffn(h.reshape(B * Lp, D), g.reshape(1, D), win_p, wout_p).reshape(B, Lp, D)

    for layer in range(depth):
        h = ffn(h, g_ffn1[layer], w_ffn1_in[layer], w_ffn1_out[layer])
        idx = layer // 2
        g = g_mix[layer].reshape(1, D)
        if layer % 2 == 0:
            w = w_attn_in[idx]
            scale = HEAD_DIM ** -0.5
            q_sb, k_sb, v_sb, q_fox, k_fox, v_fox = (
                w[:, n * d_sb:(n + 1) * d_sb] for n in range(6))
            wk = jnp.concatenate([k_sb, k_fox], axis=1).astype(bf16)
            wqvt = jnp.concatenate([q_sb * scale, v_sb, q_fox * scale, v_fox], axis=1).T.astype(bf16)
            nf = w.shape[1] - 6 * d_sb
            wf = jnp.pad(w[:, 6 * d_sb:], ((0, 0), (0, PAIR - nf))).astype(bf16)
            bfp = jnp.pad(b_forget[idx].astype(f32), (0, PAIR - nf)).reshape(1, PAIR)
            k_all, qvt, kaug, qaugt = _attn_in(h, g, wk, wqvt, wf, bfp)
            ot_sb = _sb_attn(k_all, qvt)
            ot_fox = _fox_attn(k_all, kaug, qvt, qaugt)
            wo = w_attn_out[idx].astype(bf16)
            h = _attn_out(h, ot_sb, ot_fox, wo[:d_sb], wo[d_sb:])
        else:
            h = _pool(h, g, w_pool[idx].astype(bf16), pool_scale[idx].reshape(1, D))
        h = ffn(h, g_ffn2[layer], w_ffn2_in[layer], w_ffn2_out[layer])
    return _final(h, g_final.reshape(1, D), S)
```

```python
import functools

import jax
import jax.numpy as jnp
from jax import lax
from jax.experimental import pallas as pl
from jax.experimental.pallas import tpu as pltpu

N_META = 16
N_HEADS_SB = 8
N_HEADS_FOX = 8
HEAD_DIM = 64
POOL_WINDOWS = (2, 4, 8, 16)
POOL_HALO = 16
RMS_EPS = 1e-6

BLK = 256
HALF = BLK // 2
PAIR = 2 * HEAD_DIM
FFN_CHUNK = 256
VMEM_LIMIT = 56 * 1024 * 1024

f32 = jnp.float32
bf16 = jnp.bfloat16
NT_DIMS = (((1,), (1,)), ((), ()))
TN_DIMS = (((0,), (0,)), ((), ()))


def _rms(x, g):
    return (x * lax.rsqrt(jnp.mean(x * x, axis=-1, keepdims=True) + RMS_EPS)) * g


def _log_sigmoid(z):
    return jnp.minimum(z, 0.0) - jnp.log1p(jnp.exp(-jnp.abs(z)))


def _split2(x):
    hi = x.astype(bf16)
    lo = (x - hi.astype(f32)).astype(bf16)
    return hi, lo


def _split3(x):
    hi = x.astype(bf16)
    r = x - hi.astype(f32)
    mid = r.astype(bf16)
    lo = (r - mid.astype(f32)).astype(bf16)
    return hi, mid, lo


def _dot(a, b):
    return jnp.dot(a, b, preferred_element_type=f32)


def _ffn_kernel(x_ref, g_ref, win_ref, wout_ref, o_ref, u_ref, acc_ref, *, n_chunks, tf):
    x = x_ref[...]
    u_ref[...] = _rms(x, g_ref[...]).astype(bf16)
    acc_ref[...] = jnp.zeros_like(acc_ref)

    def body(c, carry):
        gu = _dot(u_ref[...], win_ref[c])
        gate = gu[:, :tf]
        act = (gate * jax.nn.sigmoid(gate)) * gu[:, tf:]
        acc_ref[...] += _dot(act.astype(bf16), wout_ref[c])
        return carry

    lax.fori_loop(0, n_chunks, body, 0)
    o_ref[...] = x + 0.5 * acc_ref[...]


def _ffn(h2d, g, win_p, wout_p):
    T, D = h2d.shape
    n_chunks, _, tf2 = win_p.shape
    tf = tf2 // 2
    tm = 512 if T % 512 == 0 else BLK
    const = dict(pipeline_mode=pl.Buffered(1))
    return pl.pallas_call(
        functools.partial(_ffn_kernel, n_chunks=n_chunks, tf=tf),
        grid=(T // tm,),
        in_specs=[
            pl.BlockSpec((tm, D), lambda i: (i, 0)),
            pl.BlockSpec((1, D), lambda i: (0, 0)),
            pl.BlockSpec((n_chunks, D, tf2), lambda i: (0, 0, 0), **const),
            pl.BlockSpec((n_chunks, tf, D), lambda i: (0, 0, 0), **const),
        ],
        out_specs=pl.BlockSpec((tm, D), lambda i: (i, 0)),
        out_shape=jax.ShapeDtypeStruct((T, D), f32),
        scratch_shapes=[pltpu.VMEM((tm, D), bf16), pltpu.VMEM((tm, D), f32)],
        compiler_params=pltpu.CompilerParams(
            dimension_semantics=("arbitrary",), vmem_limit_bytes=VMEM_LIMIT),
        name="ffn",
    )(h2d, g, win_p, wout_p)


def _pack_ffn(w_in, w_out):
    D, two_f = w_in.shape
    F = two_f // 2
    n = F // FFN_CHUNK
    gate = w_in[:, :F].reshape(D, n, FFN_CHUNK)
    up = w_in[:, F:].reshape(D, n, FFN_CHUNK)
    win_p = jnp.concatenate([gate, up], axis=2).transpose(1, 0, 2).astype(bf16)
    wout_p = w_out.reshape(n, FFN_CHUNK, D).astype(bf16)
    return win_p, wout_p


def _attn_in_kernel(h_ref, g_ref, wk_ref, wqvt_ref, wf_ref, bf_ref,
                    k_ref, qvt_ref, kaug_ref, qaugt_ref, carry_ref):
    @pl.when(pl.program_id(1) == 0)
    def _():
        carry_ref[...] = jnp.zeros_like(carry_ref)

    u = _rms(h_ref[...], g_ref[...]).astype(bf16)
    k_ref[...] = _dot(u, wk_ref[...]).astype(bf16)
    qvt_ref[...] = lax.dot_general(wqvt_ref[...], u, NT_DIMS,
                                   preferred_element_type=f32).astype(bf16)

    lf = _log_sigmoid(_dot(u, wf_ref[...]) + bf_ref[...])
    r = lax.broadcasted_iota(jnp.int32, (BLK, BLK), 0)
    c = lax.broadcasted_iota(jnp.int32, (BLK, BLK), 1)
    ltri = jnp.where(r >= c, 1.0, 0.0).astype(bf16)
    hi, mid, lo = _split3(lf)
    cs = (_dot(ltri, hi) + _dot(ltri, mid)) + _dot(ltri, lo) + carry_ref[0:1, :]
    carry_ref[0:1, :] = cs[BLK - 1:BLK, :]

    chi, cmid, clo = _split3(cs)
    pr = lax.broadcasted_iota(jnp.int32, (PAIR, PAIR), 0)
    pc = lax.broadcasted_iota(jnp.int32, (PAIR, PAIR), 1)

    def place_k(k):
        return jnp.where((pc == 8 * pr + k) & (pr < N_HEADS_FOX), 1.0, 0.0).astype(bf16)

    def place_q(k):
        return jnp.where((pr == 8 * pc + k) & (pc < N_HEADS_FOX), 1.0, 0.0).astype(bf16)

    lane = lax.broadcasted_iota(jnp.int32, (1, PAIR), 1)
    ones_k = jnp.where(((lane & 7) < 3) & (lane < 8 * N_HEADS_FOX), 1.0, 0.0)
    kaug = ones_k - ((_dot(chi, place_k(3)) + _dot(cmid, place_k(4))) + _dot(clo, place_k(5)))
    kaug_ref[...] = kaug.astype(bf16)

    row = lax.broadcasted_iota(jnp.int32, (PAIR, 1), 0)
    ones_q = jnp.where(((row & 7) >= 3) & ((row & 7) < 6) & (row < 8 * N_HEADS_FOX), 1.0, 0.0)
    dg = functools.partial(lax.dot_general, dimension_numbers=NT_DIMS,
                           preferred_element_type=f32)
    qaugt = ones_q + ((dg(place_q(0), chi) + dg(place_q(1), cmid)) + dg(place_q(2), clo))
    qaugt_ref[...] = qaugt.astype(bf16)


def _attn_in(h, g, wk, wqvt, wf, bfp):
    B, Lp, D = h.shape
    nblk = Lp // BLK
    nk = wk.shape[1]
    nqv = wqvt.shape[0]
    const = dict(pipeline_mode=pl.Buffered(1))
    return pl.pallas_call(
        _attn_in_kernel,
        grid=(B, nblk),
        in_specs=[
            pl.BlockSpec((None, BLK, D), lambda b, j: (b, j, 0)),
            pl.BlockSpec((1, D), lambda b, j: (0, 0)),
            pl.BlockSpec((D, nk), lambda b, j: (0, 0), **const),
            pl.BlockSpec((nqv, D), lambda b, j: (0, 0), **const),
            pl.BlockSpec((D, PAIR), lambda b, j: (0, 0), **const),
            pl.BlockSpec((1, PAIR), lambda b, j: (0, 0)),
        ],
        out_specs=[
            pl.BlockSpec((None, BLK, nk), lambda b, j: (b, j, 0)),
            pl.BlockSpec((None, None, nqv, BLK), lambda b, j: (b, j, 0, 0)),
            pl.BlockSpec((None, BLK, PAIR), lambda b, j: (b, j, 0)),
            pl.BlockSpec((None, None, PAIR, BLK), lambda b, j: (b, j, 0, 0)),
        ],
        out_shape=[
            jax.ShapeDtypeStruct((B, Lp, nk), bf16),
            jax.ShapeDtypeStruct((B, nblk, nqv, BLK), bf16),
            jax.ShapeDtypeStruct((B, Lp, PAIR), bf16),
            jax.ShapeDtypeStruct((B, nblk, PAIR, BLK), bf16),
        ],
        scratch_shapes=[pltpu.VMEM((8, PAIR), f32)],
        compiler_params=pltpu.CompilerParams(
            dimension_semantics=("arbitrary", "arbitrary"), vmem_limit_bytes=VMEM_LIMIT),
        name="attn_in",
    )(h, g, wk, wqvt, wf, bfp)


NEG_LOG2E = -1.4426950408889634
SB_DONE = 110.0


def _softplus(z):
    return jnp.maximum(z, 0.0) + jnp.log(1.0 + jnp.exp2(jnp.abs(z) * NEG_LOG2E))


def _sb_kernel(k_ref, qt_ref, vt_ref, o_ref, z_s, hl_s, c_s, a_s, rp_s, acc_s):
    i = pl.program_id(2)
    qt = qt_ref[...]
    prow = lax.broadcasted_iota(jnp.int32, (PAIR, BLK), 0)
    zero = jnp.zeros_like(qt)
    qts = (jnp.where(prow < HEAD_DIM, qt, zero), jnp.where(prow >= HEAD_DIM, qt, zero))
    ur = lax.broadcasted_iota(jnp.int32, (HALF, BLK), 0)
    uc = lax.broadcasted_iota(jnp.int32, (HALF, BLK), 1)
    uu = jnp.where((uc & (HALF - 1)) >= ur, 1.0, 0.0).astype(bf16)

    rp_s[...] = jnp.zeros_like(rp_s)
    acc_s[...] = jnp.zeros_like(acc_s)

    def group(js, masks):
        nb = len(js)
        tiles = [(b, hh) for b in range(nb) for hh in range(2)]
        kblks = [k_ref[pl.ds(pl.multiple_of(j * BLK, BLK), BLK), :] for j in js]
        vblks = [vt_ref[j] for j in js]
        if any(masks):
            srow = lax.broadcasted_iota(jnp.int32, (BLK, BLK), 0)
            tcol = lax.broadcasted_iota(jnp.int32, (BLK, BLK), 1)
            valid = srow < tcol
        for b, hh in tiles:
            z_s[2 * b + hh] = _dot(kblks[b], qts[hh])
        for b, hh in tiles:
            t = 2 * b + hh
            p = _softplus(z_s[t])
            if masks[b]:
                p = jnp.where(valid, p, 0.0)
            hi, lo = _split2(p)
            hl_s[t, 0 * HALF:1 * HALF] = hi[:HALF]
            hl_s[t, 1 * HALF:2 * HALF] = lo[:HALF]
            hl_s[t, 2 * HALF:3 * HALF] = hi[HALF:]
            hl_s[t, 3 * HALF:4 * HALF] = lo[HALF:]
        for b, hh in tiles:
            t = 2 * b + hh
            c_s[t, :HALF] = _dot(uu, hl_s[t, :BLK])
            c_s[t, HALF:] = _dot(uu, hl_s[t, BLK:])
        for hh in range(2):
            rp = rp_s[hh]
            for b in range(nb):
                t = 2 * b + hh
                tot_b = c_s[t, HALF:HALF + 1]
                s_a = c_s[t, :HALF] + (rp + tot_b)
                s_b = c_s[t, HALF:] + rp
                a = jnp.exp(z_s[t] - jnp.concatenate([s_a, s_b], axis=0))
                if masks[b]:
                    a = jnp.where(valid, a, 0.0)
                a_s[t] = a.astype(bf16)
                rp = rp + (c_s[t, 0:1] + tot_b)
            rp_s[hh] = rp
        for hh in range(2):
            acc = acc_s[hh]
            for b in range(nb):
                acc = acc + _dot(vblks[b][hh * HEAD_DIM:(hh + 1) * HEAD_DIM], a_s[2 * b + hh])
            acc_s[hh] = acc

    def unfinished():
        return (jnp.minimum(jnp.min(rp_s[0]), jnp.min(rp_s[1])) < SB_DONE).astype(jnp.int32)

    @pl.when(i == 0)
    def _():
        group([i], [True])

    @pl.when(i > 0)
    def _():
        group([i, i - 1], [True, False])
        n_pairs = (i - 1) // 2

        def cond(c):
            return (c[0] < n_pairs) & (c[1] > 0)

        def body(c):
            group([i - 2 - 2 * c[0], i - 3 - 2 * c[0]], [False, False])
            return c[0] + 1, unfinished()

        done_pairs, go = lax.while_loop(cond, body, (jnp.int32(0), unfinished()))

        @pl.when((go > 0) & (done_pairs == n_pairs) & ((i - 1) % 2 == 1))
        def _():
            group([0], [False])

    o_ref[...] = jnp.concatenate([acc_s[0], acc_s[1]], axis=0).astype(bf16)


def _sb_attn(k_all, qvt):
    B, Lp, _ = k_all.shape
    nblk = Lp // BLK
    npairs = N_HEADS_SB // 2
    return pl.pallas_call(
        _sb_kernel,
        grid=(B, npairs, nblk),
        in_specs=[
            pl.BlockSpec((None, Lp, PAIR), lambda b, p, i: (b, 0, p)),
            pl.BlockSpec((None, None, PAIR, BLK), lambda b, p, i: (b, i, p, 0)),
            pl.BlockSpec((None, nblk, PAIR, BLK), lambda b, p, i: (b, 0, npairs + p, 0)),
        ],
        out_specs=pl.BlockSpec((None, None, PAIR, BLK), lambda b, p, i: (b, i, p, 0)),
        out_shape=jax.ShapeDtypeStruct((B, nblk, npairs * PAIR, BLK), bf16),
        scratch_shapes=[
            pltpu.VMEM((4, BLK, BLK), f32),
            pltpu.VMEM((4, 2 * BLK, BLK), bf16),
            pltpu.VMEM((4, BLK, BLK), f32),
            pltpu.VMEM((4, BLK, BLK), bf16),
            pltpu.VMEM((2, 1, BLK), f32),
            pltpu.VMEM((2, HEAD_DIM, BLK), f32),
        ],
        compiler_params=pltpu.CompilerParams(
            dimension_semantics=("arbitrary", "arbitrary", "arbitrary"),
            vmem_limit_bytes=VMEM_LIMIT),
        name="sb_attn",
    )(k_all, qvt, qvt)


FOX_NB = 4


def _fox_kernel(k_ref, kaug_ref, qt_ref, qaugt_ref, vt_ref, o_ref, s_s, p_s, m_s, acc_s):
    p = pl.program_id(1)
    i = pl.program_id(2)
    qt = qt_ref[...]
    qaugt = qaugt_ref[...]
    prow = lax.broadcasted_iota(jnp.int32, (PAIR, BLK), 0)
    zero = jnp.zeros_like(qt)
    rhs = []
    for hh in range(2):
        q_h = jnp.where((prow >= hh * HEAD_DIM) & (prow < (hh + 1) * HEAD_DIM), qt, zero)
        lo_row = 8 * (2 * p + hh)
        b_h = jnp.where((prow >= lo_row) & (prow < lo_row + 6), qaugt, zero)
        rhs.append(jnp.concatenate([q_h, b_h], axis=0))
    ones = jnp.ones((16, BLK), bf16)

    m_s[...] = jnp.full(m_s.shape, -jnp.inf, f32)
    acc_s[...] = jnp.zeros_like(acc_s)

    def group(js, masked):
        nb = len(js)
        tiles = [(hh, b) for hh in range(2) for b in range(nb)]
        rows = [pl.ds(pl.multiple_of(j * BLK, BLK), BLK) for j in js]
        vblks = [vt_ref[j] for j in js]
        if masked:
            srow = lax.broadcasted_iota(jnp.int32, (BLK, BLK), 0)
            tcol = lax.broadcasted_iota(jnp.int32, (BLK, BLK), 1)
            valid = srow <= tcol

        col_max = {}
        for hh, b in tiles:
            lhs = jnp.concatenate([k_ref[rows[b], :], kaug_ref[rows[b], :]], axis=1)
            st = _dot(lhs, rhs[hh])
            if masked:
                st = jnp.where(valid, st, -jnp.inf)
            s_s[nb * hh + b] = st
            col_max[hh, b] = jnp.max(st, axis=0, keepdims=True)
        for hh in range(2):
            m = m_s[hh]
            acc = acc_s[hh]
            for b in range(nb):
                m_new = jnp.maximum(m, col_max[hh, b])
                p_s[nb * hh + b] = jnp.exp(s_s[nb * hh + b] - m_new).astype(bf16)
                vl = jnp.concatenate([vblks[b][hh * HEAD_DIM:(hh + 1) * HEAD_DIM], ones], axis=0)
                acc = jnp.exp(m - m_new) * acc + _dot(vl, p_s[nb * hh + b])
                m = m_new
            acc_s[hh] = acc
            m_s[hh] = m

    group([i], True)

    def full(g, carry):
        base = i - 1 - FOX_NB * g
        group([base - n for n in range(FOX_NB)], False)
        return carry

    n_full = i // FOX_NB
    lax.fori_loop(0, n_full, full, 0)

    def single(r, carry):
        group([i - 1 - FOX_NB * n_full - r], False)
        return carry

    lax.fori_loop(0, i - FOX_NB * n_full, single, 0)

    outs = [acc_s[hh, :HEAD_DIM] / acc_s[hh, HEAD_DIM:HEAD_DIM + 1] for hh in range(2)]
    o_ref[...] = jnp.concatenate(outs, axis=0).astype(bf16)


def _fox_attn(k_all, kaug, qvt, qaugt):
    B, Lp, _ = k_all.shape
    nblk = Lp // BLK
    npairs = N_HEADS_FOX // 2
    k_off = N_HEADS_SB // 2
    q_off = 2 * (N_HEADS_SB // 2)
    v_off = q_off + npairs
    return pl.pallas_call(
        _fox_kernel,
        grid=(B, npairs, nblk),
        in_specs=[
            pl.BlockSpec((None, Lp, PAIR), lambda b, p, i: (b, 0, k_off + p)),
            pl.BlockSpec((None, Lp, PAIR), lambda b, p, i: (b, 0, 0)),
            pl.BlockSpec((None, None, PAIR, BLK), lambda b, p, i: (b, i, q_off + p, 0)),
            pl.BlockSpec((None, None, PAIR, BLK), lambda b, p, i: (b, i, 0, 0)),
            pl.BlockSpec((None, nblk, PAIR, BLK), lambda b, p, i: (b, 0, v_off + p, 0)),
        ],
        out_specs=pl.BlockSpec((None, None, PAIR, BLK), lambda b, p, i: (b, i, p, 0)),
        out_shape=jax.ShapeDtypeStruct((B, nblk, npairs * PAIR, BLK), bf16),
        scratch_shapes=[
            pltpu.VMEM((2 * FOX_NB, BLK, BLK), f32),
            pltpu.VMEM((2 * FOX_NB, BLK, BLK), bf16),
            pltpu.VMEM((2, 1, BLK), f32),
            pltpu.VMEM((2, HEAD_DIM + 16, BLK), f32),
        ],
        compiler_params=pltpu.CompilerParams(
            dimension_semantics=("arbitrary", "arbitrary", "arbitrary"),
            vmem_limit_bytes=VMEM_LIMIT),
        name="fox_attn",
    )(k_all, kaug, qvt, qaugt, qvt)


def _attn_out_kernel(h_ref, osb_ref, ofox_ref, wsb_ref, wfox_ref, o_ref):
    dg = functools.partial(lax.dot_general, dimension_numbers=TN_DIMS,
                           preferred_element_type=f32)
    o_ref[...] = h_ref[...] + (dg(osb_ref[...], wsb_ref[...]) + dg(ofox_ref[...], wfox_ref[...]))


def _attn_out(h, ot_sb, ot_fox, w_sb, w_fox):
    B, Lp, D = h.shape
    nblk = Lp // BLK
    dsb, dfox = w_sb.shape[0], w_fox.shape[0]
    const = dict(pipeline_mode=pl.Buffered(1))
    return pl.pallas_call(
        _attn_out_kernel,
        grid=(B, nblk),
        in_specs=[
            pl.BlockSpec((None, BLK, D), lambda b, j: (b, j, 0)),
            pl.BlockSpec((None, None, dsb, BLK), lambda b, j: (b, j, 0, 0)),
            pl.BlockSpec((None, None, dfox, BLK), lambda b, j: (b, j, 0, 0)),
            pl.BlockSpec((dsb, D), lambda b, j: (0, 0), **const),
            pl.BlockSpec((dfox, D), lambda b, j: (0, 0), **const),
        ],
        out_specs=pl.BlockSpec((None, BLK, D), lambda b, j: (b, j, 0)),
        out_shape=jax.ShapeDtypeStruct((B, Lp, D), f32),
        compiler_params=pltpu.CompilerParams(
            dimension_semantics=("arbitrary", "arbitrary"), vmem_limit_bytes=VMEM_LIMIT),
        name="attn_out",
    )(h, ot_sb, ot_fox, w_sb, w_fox)


def _pool_kernel(h_ref, g_ref, wp_ref, sc_ref, o_ref, ext_ref):
    j = pl.program_id(1)

    @pl.when(j == 0)
    def _():
        ext_ref[0:POOL_HALO, :] = jnp.zeros((POOL_HALO, ext_ref.shape[1]), f32)

    @pl.when(j > 0)
    def _():
        ext_ref[0:POOL_HALO, :] = ext_ref[BLK:BLK + POOL_HALO, :]

    x = h_ref[...]
    u = _rms(x, g_ref[...])
    ext_ref[POOL_HALO:, :] = u
    cg = u.shape[1] // len(POOL_WINDOWS)
    pos = j * BLK + lax.broadcasted_iota(jnp.int32, (BLK, cg), 0)
    ys = []
    for gi, w in enumerate(POOL_WINDOWS):
        cols = slice(gi * cg, (gi + 1) * cg)
        ug = u[:, cols]
        acc = ug
        for d in range(1, w):
            acc = acc + ext_ref[POOL_HALO - d:POOL_HALO - d + BLK, cols]
        cnt = jnp.minimum(pos + 1, w).astype(f32)
        pooled = acc / cnt - ug
        ys.append(_dot(pooled.astype(bf16), wp_ref[gi]))
    o_ref[...] = x + jnp.concatenate(ys, axis=1) * sc_ref[...]


def _pool(h, g, wp, sc):
    B, Lp, D = h.shape
    nblk = Lp // BLK
    ng, cg, _ = wp.shape
    return pl.pallas_call(
        _pool_kernel,
        grid=(B, nblk),
        in_specs=[
            pl.BlockSpec((None, BLK, D), lambda b, j: (b, j, 0)),
            pl.BlockSpec((1, D), lambda b, j: (0, 0)),
            pl.BlockSpec((ng, cg, cg), lambda b, j: (0, 0, 0)),
            pl.BlockSpec((1, D), lambda b, j: (0, 0)),
        ],
        out_specs=pl.BlockSpec((None, BLK, D), lambda b, j: (b, j, 0)),
        out_shape=jax.ShapeDtypeStruct((B, Lp, D), f32),
        scratch_shapes=[pltpu.VMEM((BLK + POOL_HALO, D), f32)],
        compiler_params=pltpu.CompilerParams(
            dimension_semantics=("arbitrary", "arbitrary"), vmem_limit_bytes=VMEM_LIMIT),
        name="pool_mixer",
    )(h, g, wp, sc)


def _final_kernel(h_ref, tail_ref, g_ref, o_ref):
    x = jnp.concatenate([h_ref[N_META:, :], tail_ref[...]], axis=0)
    o_ref[...] = _rms(x, g_ref[...])


def _final(h, g, seq):
    B, Lp, D = h.shape
    per = BLK // N_META
    h16 = h.reshape(B, Lp // N_META, N_META, D)
    return pl.pallas_call(
        _final_kernel,
        grid=(B, seq // BLK),
        in_specs=[
            pl.BlockSpec((None, BLK, D), lambda b, j: (b, j, 0)),
            pl.BlockSpec((None, None, N_META, D), lambda b, j: (b, (j + 1) * per, 0, 0)),
            pl.BlockSpec((1, D), lambda b, j: (0, 0)),
        ],
        out_specs=pl.BlockSpec((None, BLK, D), lambda b, j: (b, j, 0)),
        out_shape=jax.ShapeDtypeStruct((B, seq, D), f32),
        compiler_params=pltpu.CompilerParams(
            dimension_semantics=("arbitrary", "arbitrary"), vmem_limit_bytes=VMEM_LIMIT),
        name="final_norm",
    )(h, h16, g)


def kernel(x, meta_tokens, g_ffn1, w_ffn1_in, w_ffn1_out, g_mix, g_ffn2, w_ffn2_in, w_ffn2_out, w_attn_in, b_forget, w_attn_out, w_pool, pool_scale, g_final):
    B, S, D = x.shape
    depth = g_ffn1.shape[0]
    d_sb = N_HEADS_SB * HEAD_DIM
    d_fox = N_HEADS_FOX * HEAD_DIM
    assert S % BLK == 0 and D == d_sb + d_fox
    L = N_META + S
    Lp = -(-L // BLK) * BLK

    meta = jnp.broadcast_to(meta_tokens[None].astype(x.dtype), (B, N_META, D))
    h = jnp.concatenate([meta, x, jnp.zeros((B, Lp - L, D), x.dtype)], axis=1)

    def ffn(h, g, w_in, w_out):
        win_p, wout_p = _pack_ffn(w_in, w_out)
        return _ffn(h.reshape(B * Lp, D), g.reshape(1, D), win_p, wout_p).reshape(B, Lp, D)

    for layer in range(depth):
        h = ffn(h, g_ffn1[layer], w_ffn1_in[layer], w_ffn1_out[layer])
        idx = layer // 2
        g = g_mix[layer].reshape(1, D)
        if layer % 2 == 0:
            w = w_attn_in[idx]
            scale = HEAD_DIM ** -0.5
            q_sb, k_sb, v_sb, q_fox, k_fox, v_fox = (
                w[:, n * d_sb:(n + 1) * d_sb] for n in range(6))
            wk = jnp.concatenate([k_sb, k_fox], axis=1).astype(bf16)
            wqvt = jnp.concatenate([q_sb * scale, v_sb, q_fox * scale, v_fox], axis=1).T.astype(bf16)
            nf = w.shape[1] - 6 * d_sb
            wf = jnp.pad(w[:, 6 * d_sb:], ((0, 0), (0, PAIR - nf))).astype(bf16)
            bfp = jnp.pad(b_forget[idx].astype(f32), (0, PAIR - nf)).reshape(1, PAIR)
            k_all, qvt, kaug, qaugt = _attn_in(h, g, wk, wqvt, wf, bfp)
            ot_sb = _sb_attn(k_all, qvt)
            ot_fox = _fox_attn(k_all, kaug, qvt, qaugt)
            wo = w_attn_out[idx].astype(bf16)
            h = _attn_out(h, ot_sb, ot_fox, wo[:d_sb], wo[d_sb:])
        else:
            h = _pool(h, g, w_pool[idx].astype(bf16), pool_scale[idx].reshape(1, D))
        h = ffn(h, g_ffn2[layer], w_ffn2_in[layer], w_ffn2_out[layer])
    return _final(h, g_final.reshape(1, D), S)
```
